```python
import math
import jax, jax.numpy as jnp
from jax import lax
import numpy as np

D_MODEL = 1024
BATCH = 16
SEQ = 2048
DEPTH = 2

N_MIXERS = 2
N_HEADS = 16
HEAD_DIM = D_MODEL // N_HEADS
NSA_KV_GROUPS = 4
NSA_HPG = N_HEADS // NSA_KV_GROUPS
CMP_BLOCK = 32
CMP_STRIDE = 16
SEL_BLOCK = 64
SEL_TOPK = 16
N_LOCAL_BLOCKS = 2
WINDOW = 512
SEL_Q_CHUNK = 16
Q_BLOCK = 128
NSA_IN = N_HEADS * HEAD_DIM + 6 * NSA_KV_GROUPS * HEAD_DIM + 3 * N_HEADS
FOX_IN = 3 * N_HEADS * HEAD_DIM + N_HEADS
REL_BUCKETS = 32
REL_MAX_DIST = 128
D_FF = 4 * D_MODEL
RMS_EPS = 1e-6
N_NSA_LAYERS = (DEPTH + 1) // 2
N_FOX_LAYERS = DEPTH // 2
NEG_INF = -1e30
FORCE_SCORE = 1e9

kernel_name = "nsa_fox_interleaved_hybrid"


def rms_norm(x, g):
    xf = x.astype(jnp.float32)
    y = xf * lax.rsqrt(jnp.mean(xf * xf, axis=-1, keepdims=True) + RMS_EPS)
    return (y * g.astype(jnp.float32)).astype(x.dtype)


def rel_bucket(dist):
    n = jnp.maximum(dist, 0)
    max_exact = REL_BUCKETS // 2
    nf = jnp.maximum(n, 1).astype(jnp.float32)
    large = max_exact + (jnp.log(nf / max_exact) / math.log(REL_MAX_DIST / max_exact)
                         * (REL_BUCKETS - max_exact)).astype(jnp.int32)
    large = jnp.minimum(large, REL_BUCKETS - 1)
    return jnp.where(n < max_exact, n, large)


def masked_softmax(logits, mask):
    logits = jnp.where(mask, logits.astype(jnp.float32), NEG_INF)
    m = jnp.max(logits, axis=-1, keepdims=True)
    p = jnp.exp(logits - m) * mask
    return p / jnp.maximum(jnp.sum(p, axis=-1, keepdims=True), 1e-30)


def nsa_mixer(h, w_in, pe_k, wk1, wk2, pe_v, wv1, wv2, w_out, rel_bias):
    B, S, _ = h.shape
    H, G, Hg, Dh = N_HEADS, NSA_KV_GROUPS, NSA_HPG, HEAD_DIM
    proj = h @ w_in
    q = proj[..., :H * Dh].reshape(B, S, G, Hg, Dh).transpose(0, 2, 3, 1, 4) * (Dh ** -0.5)

    def kv(i):
        off = H * Dh + i * G * Dh
        return proj[..., off:off + G * Dh].reshape(B, S, G, Dh).transpose(0, 2, 1, 3)

    k_c, v_c, k_s, v_s, k_w, v_w = [kv(i) for i in range(6)]
    gates = jax.nn.sigmoid(proj[..., H * Dh + 6 * G * Dh:].astype(jnp.float32))
    gates = gates.reshape(B, S, 3, G, Hg).transpose(2, 0, 3, 4, 1)[..., None]
    t_pos = jnp.arange(S)
    g_ar = jnp.arange(G)
    rb_g = rel_bias.reshape(REL_BUCKETS, G, Hg).transpose(1, 0, 2)

    n_cmp = (S - CMP_BLOCK) // CMP_STRIDE + 1
    starts = jnp.arange(n_cmp) * CMP_STRIDE
    idx = starts[:, None] + jnp.arange(CMP_BLOCK)[None, :]

    def compress(t, pe, w1, w2):
        blocks = t[:, :, idx] + pe
        flat = blocks.reshape(B, G, n_cmp, CMP_BLOCK * Dh)
        return jax.nn.gelu(flat @ w1) @ w2

    kc = compress(k_c, pe_k, wk1, wk2)
    vc = compress(v_c, pe_v, wv1, wv2)
    dist_c = t_pos[:, None] - (starts + CMP_BLOCK - 1)[None, :]
    bias_c = rel_bias[rel_bucket(dist_c)].transpose(2, 0, 1).reshape(G, Hg, S, n_cmp)
    s_cmp = jnp.einsum('bghqd,bgcd->bghqc', q, kc).astype(jnp.float32) + bias_c
    p_cmp = masked_softmax(s_cmp, dist_c >= 0)
    o_cmp = jnp.einsum('bghqc,bgcd->bghqd', p_cmp.astype(vc.dtype), vc)

    n_sel = S // SEL_BLOCK
    n_topk = min(SEL_TOPK, n_sel)
    cells = jnp.arange(n_cmp)[:, None] + jnp.arange(CMP_BLOCK // CMP_STRIDE)[None, :]
    overlap = jnp.sum(jax.nn.one_hot(cells // (SEL_BLOCK // CMP_STRIDE), n_sel, dtype=jnp.float32), axis=1)
    imp = jnp.einsum('bghqc,cj->bgqj', p_cmp, overlap)
    j = jnp.arange(n_sel)
    rel_blk = (t_pos // SEL_BLOCK)[:, None] - j[None, :]
    forced = (j[None, :] == 0) | ((rel_blk >= 0) & (rel_blk < N_LOCAL_BLOCKS))
    visible = rel_blk >= 0
    score = jnp.where(visible, jnp.where(forced, FORCE_SCORE, imp), NEG_INF)
    _, sel_idx = lax.top_k(score, n_topk)

    kb = k_s.reshape(B, G, n_sel, SEL_BLOCK * Dh)
    vb = v_s.reshape(B, G, n_sel, SEL_BLOCK * Dh)
    n_chunk = S // SEL_Q_CHUNK
    q_ch = q.reshape(B, G, Hg, n_chunk, SEL_Q_CHUNK, Dh).transpose(3, 0, 1, 2, 4, 5)
    idx_ch = sel_idx.reshape(B, G, n_chunk, SEL_Q_CHUNK, n_topk).transpose(2, 0, 1, 3, 4)
    n_keys = n_topk * SEL_BLOCK

    def sel_chunk(args):
        qc, ic, c = args
        flat_idx = ic.reshape(B, G, SEL_Q_CHUNK * n_topk)[..., None]
        kg = jnp.take_along_axis(kb, flat_idx, axis=2).reshape(B, G, SEL_Q_CHUNK, n_keys, Dh)
        vg = jnp.take_along_axis(vb, flat_idx, axis=2).reshape(B, G, SEL_Q_CHUNK, n_keys, Dh)
        tq = c * SEL_Q_CHUNK + jnp.arange(SEL_Q_CHUNK)
        kpos = (ic[..., None] * SEL_BLOCK + jnp.arange(SEL_BLOCK)).reshape(B, G, SEL_Q_CHUNK, n_keys)
        dist = tq[None, None, :, None] - kpos
        bias = rb_g[g_ar[None, :, None, None], rel_bucket(dist)]
        s = jnp.einsum('bghqd,bgqkd->bghqk', qc, kg).astype(jnp.float32) + bias.transpose(0, 1, 4, 2, 3)
        p = masked_softmax(s, (dist >= 0)[:, :, None])
        return jnp.einsum('bghqk,bgqkd->bghqd', p.astype(vg.dtype), vg)

    o_sel = lax.map(sel_chunk, (q_ch, idx_ch, jnp.arange(n_chunk)))
    o_sel = o_sel.transpose(1, 2, 3, 0, 4, 5).reshape(B, G, Hg, S, Dh)

    n_qb = S // Q_BLOCK
    span = WINDOW + Q_BLOCK
    kp = jnp.pad(k_w, ((0, 0), (0, 0), (WINDOW, 0), (0, 0)))
    vp = jnp.pad(v_w, ((0, 0), (0, 0), (WINDOW, 0), (0, 0)))
    q_blk = q.reshape(B, G, Hg, n_qb, Q_BLOCK, Dh).transpose(3, 0, 1, 2, 4, 5)

    def win_block(args):
        qb_, i = args
        start = i * Q_BLOCK
        kblk = lax.dynamic_slice_in_dim(kp, start, span, axis=2)
        vblk = lax.dynamic_slice_in_dim(vp, start, span, axis=2)
        tq = start + jnp.arange(Q_BLOCK)
        kpos = start - WINDOW + jnp.arange(span)
        dist = tq[:, None] - kpos[None, :]
        mask = (dist >= 0) & (dist < WINDOW) & (kpos[None, :] >= 0)
        bias = rel_bias[rel_bucket(dist)].transpose(2, 0, 1).reshape(G, Hg, Q_BLOCK, span)
        s = jnp.einsum('bghqd,bgkd->bghqk', qb_, kblk).astype(jnp.float32) + bias
        p = masked_softmax(s, mask)
        return jnp.einsum('bghqk,bgkd->bghqd', p.astype(vblk.dtype), vblk)

    o_win = lax.map(win_block, (q_blk, jnp.arange(n_qb)))
    o_win = o_win.transpose(1, 2, 3, 0, 4, 5).reshape(B, G, Hg, S, Dh)

    o = (gates[0] * o_cmp + gates[1] * o_sel + gates[2] * o_win).astype(h.dtype)
    o = o.transpose(0, 3, 1, 2, 4).reshape(B, S, H * Dh)
    return o @ w_out


def fox_mixer(h, w_in, b_f, w_out):
    B, S, _ = h.shape
    H, Dh = N_HEADS, HEAD_DIM
    proj = h @ w_in

    def heads(i):
        return proj[..., i * H * Dh:(i + 1) * H * Dh].reshape(B, S, H, Dh).transpose(0, 2, 1, 3)

    q = heads(0) * (Dh ** -0.5)
    k = heads(1)
    v = heads(2)
    f_logit = proj[..., 3 * H * Dh:].astype(jnp.float32) + b_f.astype(jnp.float32)
    c = jnp.cumsum(jax.nn.log_sigmoid(f_logit), axis=1).transpose(0, 2, 1)
    n_qb = S // Q_BLOCK
    q_blk = q.reshape(B, H, n_qb, Q_BLOCK, Dh).transpose(2, 0, 1, 3, 4)
    c_blk = c.reshape(B, H, n_qb, Q_BLOCK).transpose(2, 0, 1, 3)
    kpos = jnp.arange(S)

    def blk(args):
        qb_, cq, i = args
        tq = i * Q_BLOCK + jnp.arange(Q_BLOCK)
        s = (jnp.einsum('bhqd,bhkd->bhqk', qb_, k).astype(jnp.float32)
             + cq[..., None] - c[:, :, None, :])
        p = masked_softmax(s, tq[:, None] >= kpos[None, :])
        return jnp.einsum('bhqk,bhkd->bhqd', p.astype(v.dtype), v)

    o = lax.map(blk, (q_blk, c_blk, jnp.arange(n_qb)))
    o = o.transpose(1, 0, 3, 2, 4).reshape(B, S, H * Dh)
    return o @ w_out


def sq_relu_mlp(h, w1, w2):
    a = jax.nn.relu(h @ w1)
    return (a * a) @ w2


def setup_inputs(seed: int = 0) -> dict:
    key = jax.random.key(seed)
    ks = jax.random.split(key, 24)
    f32 = jnp.float32
    Dh = HEAD_DIM

    def nrm(k, shape, scale):
        return jax.random.normal(k, shape, f32) * scale

    return {
        "x": nrm(ks[0], (BATCH, SEQ, D_MODEL), 1.0),
        "rel_bias": nrm(ks[1], (REL_BUCKETS, N_HEADS), 0.2),
        "norm_mix": 1.0 + nrm(ks[2], (DEPTH, D_MODEL), 0.05),
        "norm_mlp": 1.0 + nrm(ks[3], (DEPTH, D_MODEL), 0.05),
        "nsa_w_in": nrm(ks[4], (N_NSA_LAYERS, D_MODEL, NSA_IN), D_MODEL ** -0.5),
        "nsa_pe_k": nrm(ks[5], (N_NSA_LAYERS, CMP_BLOCK, Dh), 0.1),
        "nsa_wk1": nrm(ks[6], (N_NSA_LAYERS, CMP_BLOCK * Dh, Dh), (CMP_BLOCK * Dh) ** -0.5),
        "nsa_wk2": nrm(ks[7], (N_NSA_LAYERS, Dh, Dh), Dh ** -0.5),
        "nsa_pe_v": nrm(ks[8], (N_NSA_LAYERS, CMP_BLOCK, Dh), 0.1),
        "nsa_wv1": nrm(ks[9], (N_NSA_LAYERS, CMP_BLOCK * Dh, Dh), (CMP_BLOCK * Dh) ** -0.5),
        "nsa_wv2": nrm(ks[10], (N_NSA_LAYERS, Dh, Dh), Dh ** -0.5),
        "nsa_w_out": nrm(ks[11], (N_NSA_LAYERS, N_HEADS * Dh, D_MODEL), (N_HEADS * Dh) ** -0.5),
        "fox_w_in": nrm(ks[12], (N_FOX_LAYERS, D_MODEL, FOX_IN), D_MODEL ** -0.5),
        "fox_b_f": jax.random.uniform(ks[13], (N_FOX_LAYERS, N_HEADS), f32, 1.0, 4.0),
        "fox_w_out": nrm(ks[14], (N_FOX_LAYERS, N_HEADS * Dh, D_MODEL), (N_HEADS * Dh) ** -0.5),
        "mlp_w1": nrm(ks[15], (DEPTH, D_MODEL, D_FF), D_MODEL ** -0.5),
        "mlp_w2": nrm(ks[16], (DEPTH, D_FF, D_MODEL), D_FF ** -0.5),
        "final_norm": 1.0 + nrm(ks[17], (D_MODEL,), 0.05),
    }


def reference(x, rel_bias, norm_mix, norm_mlp, nsa_w_in, nsa_pe_k, nsa_wk1, nsa_wk2,
              nsa_pe_v, nsa_wv1, nsa_wv2, nsa_w_out, fox_w_in, fox_b_f, fox_w_out,
              mlp_w1, mlp_w2, final_norm):
    for i in range(DEPTH):
        hn = rms_norm(x, norm_mix[i])
        li = i // N_MIXERS
        if i % N_MIXERS == 0:
            mix = nsa_mixer(hn, nsa_w_in[li], nsa_pe_k[li], nsa_wk1[li], nsa_wk2[li],
                            nsa_pe_v[li], nsa_wv1[li], nsa_wv2[li], nsa_w_out[li], rel_bias)
        else:
            mix = fox_mixer(hn, fox_w_in[li], fox_b_f[li], fox_w_out[li])
        x = x + mix.astype(x.dtype)
        hn = rms_norm(x, norm_mlp[i])
        x = x + sq_relu_mlp(hn, mlp_w1[i], mlp_w2[i]).astype(x.dtype)
    return rms_norm(x, final_norm)
```

```python
import functools
import math

import numpy as np
import jax
import jax.numpy as jnp
from jax import lax
from jax.experimental import pallas as pl
from jax.experimental.pallas import tpu as pltpu

F32 = jnp.float32
BF16 = jnp.bfloat16

D_MODEL = 1024
SEQ = 2048
N_HEADS = 16
HEAD_DIM = 64
N_GROUPS = 4
HPG = N_HEADS // N_GROUPS
CMP_BLOCK = 32
CMP_STRIDE = 16
N_CMP = (SEQ - CMP_BLOCK) // CMP_STRIDE + 1
N_CMP_PAD = 128
SEL_BLOCK = 64
N_SEL = SEQ // SEL_BLOCK
SEL_TOPK = 16
N_LOCAL_BLOCKS = 2
WINDOW = 512
REL_BUCKETS = 32
REL_MAX_DIST = 128
D_FF = 4 * D_MODEL
RMS_EPS = 1e-6
NEG = -1e30
FORCE_SCORE = 1e9

LANES = 128
TQ = 128
NQ = SEQ // TQ
WIN_TILES = WINDOW // TQ
FOX_T = 256
VMEM_LIMIT = 48 * 1024 * 1024

NSA_GROUP_COLS = 6 * LANES
NSA_KC_OFF = N_GROUPS * NSA_GROUP_COLS
NSA_VC_OFF = NSA_KC_OFF + N_GROUPS * HEAD_DIM
NSA_MAIN_COLS = NSA_VC_OFF + N_GROUPS * HEAD_DIM
NSA_GATE_OFF = N_HEADS * HEAD_DIM + 6 * N_GROUPS * HEAD_DIM
FOX_MAIN_COLS = 3 * N_HEADS * HEAD_DIM


def _nn(a, b):
    return jnp.dot(a, b, preferred_element_type=F32)


def _nt(a, b):
    return lax.dot_general(a, b, (((1,), (1,)), ((), ())), preferred_element_type=F32)


def _rms(x, g):
    ms = jnp.mean(x * x, axis=-1, keepdims=True)
    return x * lax.rsqrt(ms + RMS_EPS) * g


def _params(*sem):
    return pltpu.CompilerParams(dimension_semantics=sem, vmem_limit_bytes=VMEM_LIMIT)


def _norm_matmul_kernel(x_ref, g_ref, w_ref, o_ref, hn_ref):
    @pl.when(pl.program_id(1) == 0)
    def _():
        hn_ref[...] = _rms(x_ref[...], g_ref[...]).astype(BF16)

    o_ref[...] = _nn(hn_ref[...], w_ref[...]).astype(o_ref.dtype)


def _norm_matmul(x, g, w, out_dtype, tm, tn):
    t, d = x.shape
    n = w.shape[1]
    return pl.pallas_call(
        _norm_matmul_kernel,
        grid=(t // tm, n // tn),
        in_specs=[
            pl.BlockSpec((tm, d), lambda i, j: (i, 0)),
            pl.BlockSpec((1, d), lambda i, j: (0, 0)),
            pl.BlockSpec((d, tn), lambda i, j: (0, j)),
        ],
        out_specs=pl.BlockSpec((tm, tn), lambda i, j: (i, j)),
        out_shape=jax.ShapeDtypeStruct((t, n), out_dtype),
        scratch_shapes=[pltpu.VMEM((tm, d), BF16)],
        compiler_params=_params("parallel", "arbitrary"),
        name="norm_matmul",
    )(x, g, w)


def _matmul_res_kernel(a_ref, w_ref, r_ref, o_ref):
    o_ref[...] = r_ref[...] + _nn(a_ref[...], w_ref[...])


def _matmul_res(a, w, res, tm, tn):
    t, k = a.shape
    n = w.shape[1]
    return pl.pallas_call(
        _matmul_res_kernel,
        grid=(t // tm, n // tn),
        in_specs=[
            pl.BlockSpec((tm, k), lambda i, j: (i, 0)),
            pl.BlockSpec((k, tn), lambda i, j: (0, j)),
            pl.BlockSpec((tm, tn), lambda i, j: (i, j)),
        ],
        out_specs=pl.BlockSpec((tm, tn), lambda i, j: (i, j)),
        out_shape=jax.ShapeDtypeStruct((t, n), F32),
        compiler_params=_params("parallel", "arbitrary"),
        name="matmul_res",
    )(a, w, res)


def _mlp_kernel(x_ref, g_ref, w1_ref, w2_ref, gf_ref, o_ref, hn_ref, acc_ref, *, final):
    j = pl.program_id(1)

    @pl.when(j == 0)
    def _():
        hn_ref[...] = _rms(x_ref[...], g_ref[...]).astype(BF16)
        acc_ref[...] = jnp.zeros_like(acc_ref)

    a = jnp.maximum(_nn(hn_ref[...], w1_ref[...]), 0.0)
    acc_ref[...] += _nn((a * a).astype(BF16), w2_ref[...])

    @pl.when(j == pl.num_programs(1) - 1)
    def _():
        y = x_ref[...] + acc_ref[...]
        if final:
            y = _rms(y, gf_ref[...])
        o_ref[...] = y


def _mlp(x, g, w1, w2, gf, final, tm, tf):
    t, d = x.shape
    f = w1.shape[1]
    return pl.pallas_call(
        functools.partial(_mlp_kernel, final=final),
        grid=(t // tm, f // tf),
        in_specs=[
            pl.BlockSpec((tm, d), lambda i, j: (i, 0)),
            pl.BlockSpec((1, d), lambda i, j: (0, 0)),
            pl.BlockSpec((d, tf), lambda i, j: (0, j)),
            pl.BlockSpec((tf, d), lambda i, j: (j, 0)),
            pl.BlockSpec((1, d), lambda i, j: (0, 0)),
        ],
        out_specs=pl.BlockSpec((tm, d), lambda i, j: (i, 0)),
        out_shape=jax.ShapeDtypeStruct((t, d), F32),
        scratch_shapes=[pltpu.VMEM((tm, d), BF16), pltpu.VMEM((tm, d), F32)],
        compiler_params=_params("parallel", "arbitrary"),
        name="mlp",
    )(x, g, w1, w2, gf)


def _gelu_tanh(x):
    return 0.5 * x * (1.0 + jnp.tanh(math.sqrt(2.0 / math.pi) * (x + 0.044715 * (x * x * x))))


def _compress_one(t_ref, pe_ref, w1_ref, w2_ref, o_ref):
    t = t_ref[0, 0].astype(F32)
    xa = (t + pe_ref[0:1, :]).astype(BF16)
    xb = (t + pe_ref[1:2, :]).astype(BF16)
    top = _nn(xa, w1_ref[0])
    bot = _nn(xb, w1_ref[1])
    pre = top + pltpu.roll(bot, N_CMP_PAD - 1, axis=0)
    o_ref[0, 0] = _nn(_gelu_tanh(pre).astype(BF16), w2_ref[...]).astype(o_ref.dtype)


def _compress_kernel(tk_ref, tv_ref, pek_ref, pev_ref, wk1_ref, wv1_ref, wk2_ref, wv2_ref,
                     kc_ref, vc_ref):
    _compress_one(tk_ref, pek_ref, wk1_ref, wk2_ref, kc_ref)
    _compress_one(tv_ref, pev_ref, wv1_ref, wv2_ref, vc_ref)


def _compress(tk, tv, pek, pev, wk1, wv1, wk2, wv2):
    b = tk.shape[0]
    chunk = CMP_STRIDE * HEAD_DIM
    t_spec = pl.BlockSpec((1, 1, N_CMP_PAD, chunk), lambda i, j: (i, j, 0, 0))
    pe_spec = pl.BlockSpec((2, chunk), lambda i, j: (0, 0))
    w1_spec = pl.BlockSpec((2, chunk, HEAD_DIM), lambda i, j: (0, 0, 0))
    w2_spec = pl.BlockSpec((HEAD_DIM, LANES), lambda i, j: (0, 0))
    o_spec = pl.BlockSpec((1, 1, N_CMP_PAD, LANES), lambda i, j: (i, j, 0, 0))
    o_shape = jax.ShapeDtypeStruct((b, N_GROUPS, N_CMP_PAD, LANES), BF16)
    return pl.pallas_call(
        _compress_kernel,
        grid=(b, N_GROUPS),
        in_specs=[t_spec, t_spec, pe_spec, pe_spec, w1_spec, w1_spec, w2_spec, w2_spec],
        out_specs=[o_spec, o_spec],
        out_shape=[o_shape, o_shape],
        compiler_params=_params("parallel", "arbitrary"),
        name="nsa_compress",
    )(tk, tv, pek, pev, wk1, wv1, wk2, wv2)


def _online_update(s, v, m_ref, l_ref, acc_ref):
    m_prev = m_ref[...]
    m_new = jnp.maximum(m_prev, jnp.max(s, axis=1, keepdims=True))
    alpha = jnp.exp(m_prev - m_new)
    p = jnp.exp(s - m_new)
    l_ref[...] = alpha * l_ref[...] + jnp.sum(p, axis=1, keepdims=True)
    acc_ref[...] = alpha * acc_ref[...] + _nn(p.astype(BF16), v)
    m_ref[...] = m_new


def _online_init(m_ref, l_ref, acc_ref):
    m_ref[...] = jnp.full(m_ref.shape, -jnp.inf, F32)
    l_ref[...] = jnp.zeros(l_ref.shape, F32)
    acc_ref[...] = jnp.zeros(acc_ref.shape, F32)


def _nsa_attn_kernel(pq_ref, kc_ref, vc_ref, bc_ref, bt_ref, gate_ref, epat_ref, ovt_ref,
                     o_ref, m_ref, l_ref, acc_ref):
    i = pl.program_id(2)
    r0 = pl.multiple_of(i * TQ, TQ)
    lane = lax.broadcasted_iota(jnp.int32, (TQ, LANES), 1)
    lo = lane < HEAD_DIM

    q01 = pq_ref[pl.ds(r0, TQ), 0:LANES].astype(F32)
    q23 = pq_ref[pl.ds(r0, TQ), LANES:2 * LANES].astype(F32)
    q4 = jnp.concatenate(
        [jnp.where(lo, q01, 0.0), jnp.where(lo, 0.0, q01),
         jnp.where(lo, q23, 0.0), jnp.where(lo, 0.0, q23)], axis=0).astype(BF16)

    s = _nt(q4, kc_ref[0, 0]) + bc_ref[0, 0]
    m = jnp.max(s, axis=1, keepdims=True)
    e = jnp.where(s > 0.5 * NEG, jnp.exp(s - m), 0.0)
    p = e / jnp.maximum(jnp.sum(e, axis=1, keepdims=True), 1e-30)
    pb = p.astype(BF16)
    o_cmp = _nn(pb, vc_ref[0, 0])

    pcat = jnp.concatenate([pb[h * TQ:(h + 1) * TQ] for h in range(HPG)], axis=1)
    imp_t = _nt(ovt_ref[...], pcat)
    jidx = lax.broadcasted_iota(jnp.int32, (N_SEL, TQ), 0)
    tpos = i * TQ + lax.broadcasted_iota(jnp.int32, (N_SEL, TQ), 1)
    rel = (tpos >> int(math.log2(SEL_BLOCK))) - jidx
    forced = (jidx == 0) | ((rel >= 0) & (rel < N_LOCAL_BLOCKS))
    score = jnp.where(rel >= 0, jnp.where(forced, FORCE_SCORE, imp_t), NEG)
    rank = jnp.zeros((N_SEL, TQ), F32)
    for jp in range(N_SEL):
        sj = score[jp:jp + 1, :]
        before = (sj > score) | ((sj == score) & (jidx > jp))
        rank = rank + jnp.where(before, 1.0, 0.0)
    drop_t = jnp.where(rank < SEL_TOPK, 0.0, -1.0)
    drop_t = jnp.concatenate([drop_t, jnp.zeros((TQ - N_SEL, TQ), F32)], axis=0).astype(BF16)
    eye = (lax.broadcasted_iota(jnp.int32, (TQ, TQ), 0)
           == lax.broadcasted_iota(jnp.int32, (TQ, TQ), 1)).astype(F32).astype(BF16)
    drop = _nt(eye, drop_t).astype(BF16)
    q4x = jnp.concatenate([q4, jnp.concatenate([drop] * HPG, axis=0)], axis=1)

    _online_init(m_ref, l_ref, acc_ref)

    def sel_body(kt, carry):
        k0 = pl.multiple_of(kt * TQ, TQ)
        ks = pq_ref[pl.ds(k0, TQ), 2 * LANES:3 * LANES]
        vs = pq_ref[pl.ds(k0, TQ), 3 * LANES:4 * LANES]
        kx = jnp.concatenate([ks, epat_ref[pl.ds(k0, TQ), :]], axis=1)
        sc = _nt(q4x, kx) + bt_ref[0, jnp.minimum(i - kt, 2)]
        _online_update(sc, vs, m_ref, l_ref, acc_ref)
        return carry

    lax.fori_loop(0, i + 1, sel_body, 0)
    o_sel = acc_ref[...] / l_ref[...]

    _online_init(m_ref, l_ref, acc_ref)

    def win_body(kt, carry):
        k0 = pl.multiple_of(kt * TQ, TQ)
        kw = pq_ref[pl.ds(k0, TQ), 4 * LANES:5 * LANES]
        vw = pq_ref[pl.ds(k0, TQ), 5 * LANES:6 * LANES]
        d = i - kt
        sc = _nt(q4, kw) + bt_ref[0, jnp.where(d == WIN_TILES, 3, jnp.minimum(d, 2))]
        _online_update(sc, vw, m_ref, l_ref, acc_ref)
        return carry

    lax.fori_loop(jnp.maximum(i - WIN_TILES, 0), i + 1, win_body, 0)
    o_win = acc_ref[...] / l_ref[...]

    gl = gate_ref[0, 0]
    gs = 1.0 / (1.0 + jnp.exp(-gl))

    def merged(h):
        rows = slice(h * TQ, (h + 1) * TQ)
        return (gs[:, h:h + 1] * o_cmp[rows] + gs[:, HPG + h:HPG + h + 1] * o_sel[rows]
                + gs[:, 2 * HPG + h:2 * HPG + h + 1] * o_win[rows])

    o_ref[...] = jnp.concatenate(
        [jnp.where(lo, merged(0), merged(1)), jnp.where(lo, merged(2), merged(3))],
        axis=1).astype(o_ref.dtype)


def _nsa_attn(proj, kcd, vcd, bias_c, bias_t, gates, epat, ovt):
    b = kcd.shape[0]
    rows = HPG * TQ
    return pl.pallas_call(
        _nsa_attn_kernel,
        grid=(N_GROUPS, b, NQ),
        in_specs=[
            pl.BlockSpec((SEQ, NSA_GROUP_COLS), lambda g, bi, i: (bi, g)),
            pl.BlockSpec((1, 1, N_CMP_PAD, LANES), lambda g, bi, i: (bi, g, 0, 0)),
            pl.BlockSpec((1, 1, N_CMP_PAD, LANES), lambda g, bi, i: (bi, g, 0, 0)),
            pl.BlockSpec((1, 1, rows, N_CMP_PAD), lambda g, bi, i: (g, i, 0, 0)),
            pl.BlockSpec((1, 4, rows, TQ), lambda g, bi, i: (g, 0, 0, 0)),
            pl.BlockSpec((1, 1, TQ, 3 * HPG), lambda g, bi, i: (bi, g, i, 0)),
            pl.BlockSpec((SEQ, LANES), lambda g, bi, i: (0, 0)),
            pl.BlockSpec((N_SEL, HPG * N_CMP_PAD), lambda g, bi, i: (0, 0)),
        ],
        out_specs=pl.BlockSpec((TQ, HPG * HEAD_DIM), lambda g, bi, i: (bi * NQ + i, g)),
        out_shape=jax.ShapeDtypeStruct((b * SEQ, N_HEADS * HEAD_DIM), BF16),
        scratch_shapes=[pltpu.VMEM((rows, 1), F32), pltpu.VMEM((rows, 1), F32),
                        pltpu.VMEM((rows, LANES), F32)],
        compiler_params=_params("parallel", "parallel", "arbitrary"),
        name="nsa_attn",
    )(proj, kcd, vcd, bias_c, bias_t, gates, epat, ovt)


def _fox_cumsum_kernel(f_ref, b_ref, c_ref):
    x = f_ref[...] + b_ref[...]
    ls = -(jnp.maximum(-x, 0.0) + jnp.log1p(jnp.exp(-jnp.abs(x))))
    tri = (lax.broadcasted_iota(jnp.int32, (LANES, LANES), 0)
           >= lax.broadcasted_iota(jnp.int32, (LANES, LANES), 1)).astype(F32).astype(BF16)
    carry = jnp.zeros((1, LANES), F32)
    for blk in range(SEQ // LANES):
        xb = ls[blk * LANES:(blk + 1) * LANES]
        hi = xb.astype(BF16)
        r1 = xb - hi.astype(F32)
        mid = r1.astype(BF16)
        low = (r1 - mid.astype(F32)).astype(BF16)
        cs = _nn(tri, hi) + _nn(tri, mid) + _nn(tri, low) + carry
        c_ref[blk * LANES:(blk + 1) * LANES, :] = cs
        carry = cs[LANES - 1:LANES, :]


def _fox_cumsum(fl, bf, b):
    return pl.pallas_call(
        _fox_cumsum_kernel,
        grid=(b,),
        in_specs=[pl.BlockSpec((SEQ, LANES), lambda i: (i, 0)),
                  pl.BlockSpec((1, LANES), lambda i: (0, 0))],
        out_specs=pl.BlockSpec((SEQ, LANES), lambda i: (i, 0)),
        out_shape=jax.ShapeDtypeStruct((b * SEQ, LANES), F32),
        compiler_params=_params("parallel"),
        name="fox_cumsum",
    )(fl, bf)


def _fox_attn_kernel(q_ref, k_ref, v_ref, cq_ref, ck_ref, o_ref, m_ref, l_ref, acc_ref):
    i = pl.program_id(2)
    t = FOX_T
    lane = lax.broadcasted_iota(jnp.int32, (t, LANES), 1)
    lo = lane < HEAD_DIM
    q2 = q_ref[...].astype(F32)
    qs = jnp.concatenate([jnp.where(lo, q2, 0.0), jnp.where(lo, 0.0, q2)], axis=0).astype(BF16)
    cq = cq_ref[0, 0]
    cqs = jnp.concatenate([cq[:, 0:1], cq[:, 1:2]], axis=0)
    row = lax.broadcasted_iota(jnp.int32, (2 * t, t), 0) & (t - 1)
    col = lax.broadcasted_iota(jnp.int32, (2 * t, t), 1)
    ahead = col - row

    _online_init(m_ref, l_ref, acc_ref)

    def body(kt, carry):
        k0 = pl.multiple_of(kt * t, t)
        k = k_ref[pl.ds(k0, t), :]
        v = v_ref[pl.ds(k0, t), :]
        ck = ck_ref[0, 0, :, pl.ds(k0, t)]
        cks = jnp.concatenate([jnp.broadcast_to(ck[0:1], (t, t)),
                               jnp.broadcast_to(ck[1:2], (t, t))], axis=0)
        s = _nt(qs, k) + cqs - cks
        s = jnp.where(ahead <= (i - kt) * t, s, NEG)
        _online_update(s, v, m_ref, l_ref, acc_ref)
        return carry

    lax.fori_loop(0, i + 1, body, 0)
    o = acc_ref[...] / l_ref[...]
    o_ref[...] = jnp.where(lo, o[0:t], o[t:2 * t]).astype(o_ref.dtype)


def _fox_attn(proj, cq, ck):
    b = cq.shape[0]
    t = FOX_T
    nq = SEQ // t
    pairs = N_HEADS // 2
    return pl.pallas_call(
        _fox_attn_kernel,
        grid=(b, pairs, nq),
        in_specs=[
            pl.BlockSpec((t, LANES), lambda bi, hp, i: (bi * nq + i, hp)),
            pl.BlockSpec((SEQ, LANES), lambda bi, hp, i: (bi, pairs + hp)),
            pl.BlockSpec((SEQ, LANES), lambda bi, hp, i: (bi, 2 * pairs + hp)),
            pl.BlockSpec((1, 1, t, 2), lambda bi, hp, i: (bi, hp, i, 0)),
            pl.BlockSpec((1, 1, 2, SEQ), lambda bi, hp, i: (bi, hp, 0, 0)),
        ],
        out_specs=pl.BlockSpec((t, LANES), lambda bi, hp, i: (bi * nq + i, hp)),
        out_shape=jax.ShapeDtypeStruct((b * SEQ, N_HEADS * HEAD_DIM), BF16),
        scratch_shapes=[pltpu.VMEM((2 * t, 1), F32), pltpu.VMEM((2 * t, 1), F32),
                        pltpu.VMEM((2 * t, LANES), F32)],
        compiler_params=_params("parallel", "parallel", "arbitrary"),
        name="fox_attn",
    )(proj, proj, proj, cq, ck)


def _rel_bucket(dist):
    n = jnp.maximum(dist, 0)
    max_exact = REL_BUCKETS // 2
    nf = jnp.maximum(n, 1).astype(F32)
    large = max_exact + (jnp.log(nf / max_exact) / math.log(REL_MAX_DIST / max_exact)
                         * (REL_BUCKETS - max_exact)).astype(jnp.int32)
    large = jnp.minimum(large, REL_BUCKETS - 1)
    return jnp.where(n < max_exact, n, large)


def _group_rows(tab):
    lead = tab.shape[:-3]
    r, c = tab.shape[-3], tab.shape[-2]
    nl = len(lead)
    tab = tab.reshape(lead + (r, c, N_GROUPS, HPG))
    perm = (nl + 2,) + tuple(range(nl)) + (nl + 3, nl, nl + 1)
    return tab.transpose(perm).reshape((N_GROUPS,) + lead + (HPG * r, c))


def _bias_tables(rel_bias):
    fb = rel_bias[_rel_bucket(jnp.arange(SEQ))].astype(F32)
    neg = jnp.full((SEQ, N_HEADS), NEG, F32)
    fpad = jnp.concatenate([neg, fb], axis=0)
    cols = []
    for c in range(N_CMP_PAD):
        end = c * CMP_STRIDE + CMP_BLOCK - 1
        cols.append(fpad[SEQ - end:2 * SEQ - end] if c < N_CMP else neg)
    bc = jnp.stack(cols, axis=1).reshape(NQ, TQ, N_CMP_PAD, N_HEADS)
    bias_c = _group_rows(bc)
    r = jnp.arange(TQ)[:, None]
    c = jnp.arange(TQ)[None, :]
    t0 = fpad[SEQ + r - c]
    t1 = fpad[SEQ + TQ + r - c]
    far = jnp.broadcast_to(fb[SEQ - 1], (TQ, TQ, N_HEADS))
    t3 = jnp.where((r < c)[..., None], far, NEG)
    bias_t = _group_rows(jnp.stack([t0, t1, far, t3], axis=0))
    return bias_c, bias_t


def _nsa_const_tables():
    k = np.arange(SEQ)[:, None]
    j = np.arange(LANES)[None, :]
    epat = np.where(k // SEL_BLOCK == j, -NEG, 0.0).astype(np.float32)
    c = np.arange(N_CMP_PAD)[None, :]
    jj = np.arange(N_SEL)[:, None]
    per = SEL_BLOCK // CMP_STRIDE
    ov = ((c // per == jj).astype(np.float32) + ((c + 1) // per == jj).astype(np.float32))
    ov = np.where(c < N_CMP, ov, 0.0)
    return jnp.asarray(epat, BF16), jnp.asarray(np.tile(ov, (1, HPG)), BF16)


def _nsa_w_in_layout(w_in):
    scale = HEAD_DIM ** -0.5
    hd = HEAD_DIM
    kv_off = N_HEADS * hd

    def kv(idx, g):
        o = kv_off + idx * N_GROUPS * hd + g * hd
        return w_in[:, o:o + hd]

    parts = []
    for g in range(N_GROUPS):
        parts.append(w_in[:, g * HPG * hd:(g + 1) * HPG * hd] * scale)
        for idx in (2, 3, 4, 5):
            parts += [kv(idx, g), kv(idx, g)]
    parts += [kv(0, g) for g in range(N_GROUPS)] + [kv(1, g) for g in range(N_GROUPS)]
    w_main = jnp.concatenate(parts, axis=1).astype(BF16)
    w_gate = jnp.pad(w_in[:, NSA_GATE_OFF:], ((0, 0), (0, LANES - 3 * N_HEADS))).astype(BF16)
    return w_main, w_gate


def _nsa_layer(x, b, g_mix, rel_bias, w_in, pe_k, wk1, wk2, pe_v, wv1, wv2, w_out):
    w_main, w_gate = _nsa_w_in_layout(w_in)
    proj = _norm_matmul(x, g_mix, w_main, BF16, 1024, 512)
    gl = _norm_matmul(x, g_mix, w_gate, F32, 1024, LANES)

    chunk = CMP_STRIDE * HEAD_DIM

    def chunks(off):
        t = proj[:, off:off + N_GROUPS * HEAD_DIM].reshape(b, SEQ, N_GROUPS, HEAD_DIM)
        return t.transpose(0, 2, 1, 3).reshape(b, N_GROUPS, N_CMP_PAD, chunk)

    def dup(w):
        return jnp.concatenate([w, w], axis=1).astype(BF16)

    kcd, vcd = _compress(
        chunks(NSA_KC_OFF), chunks(NSA_VC_OFF),
        pe_k.reshape(2, chunk), pe_v.reshape(2, chunk),
        wk1.reshape(2, chunk, HEAD_DIM).astype(BF16), wv1.reshape(2, chunk, HEAD_DIM).astype(BF16),
        dup(wk2), dup(wv2))

    gates = gl[:, :3 * N_HEADS].reshape(b, SEQ, 3, N_GROUPS, HPG)
    gates = gates.transpose(0, 3, 1, 2, 4).reshape(b, N_GROUPS, SEQ, 3 * HPG)
    bias_c, bias_t = _bias_tables(rel_bias)
    epat, ovt = _nsa_const_tables()
    o = _nsa_attn(proj, kcd, vcd, bias_c, bias_t, gates, epat, ovt)
    return _matmul_res(o, w_out.astype(BF16), x, 1024, 512)


def _fox_layer(x, b, g_mix, w_in, b_f, w_out):
    hd = N_HEADS * HEAD_DIM
    w_main = jnp.concatenate([w_in[:, :hd] * HEAD_DIM ** -0.5, w_in[:, hd:3 * hd]], axis=1).astype(BF16)
    w_f = jnp.pad(w_in[:, 3 * hd:], ((0, 0), (0, LANES - N_HEADS))).astype(BF16)
    proj = _norm_matmul(x, g_mix, w_main, BF16, 1024, 512)
    fl = _norm_matmul(x, g_mix, w_f, F32, 1024, LANES)
    bf = jnp.pad(b_f.astype(F32), (0, LANES - N_HEADS)).reshape(1, LANES)
    c = _fox_cumsum(fl, bf, b)[:, :N_HEADS].reshape(b, SEQ, N_HEADS // 2, 2)
    cq = c.transpose(0, 2, 1, 3)
    ck = c.transpose(0, 2, 3, 1)
    o = _fox_attn(proj, cq, ck)
    return _matmul_res(o, w_out.astype(BF16), x, 1024, 512)


def kernel(x, rel_bias, norm_mix, norm_mlp, nsa_w_in, nsa_pe_k, nsa_wk1, nsa_wk2, nsa_pe_v,
           nsa_wv1, nsa_wv2, nsa_w_out, fox_w_in, fox_b_f, fox_w_out, mlp_w1, mlp_w2, final_norm):
    b, s, d = x.shape
    assert s == SEQ and d == D_MODEL
    depth = norm_mix.shape[0]
    h = x.reshape(b * s, d)
    gf = final_norm.reshape(1, d)
    for i in range(depth):
        li = i // 2
        g_mix = norm_mix[i].reshape(1, d)
        if i % 2 == 0:
            h = _nsa_layer(h, b, g_mix, rel_bias, nsa_w_in[li], nsa_pe_k[li], nsa_wk1[li],
                           nsa_wk2[li], nsa_pe_v[li], nsa_wv1[li], nsa_wv2[li], nsa_w_out[li])
        else:
            h = _fox_layer(h, b, g_mix, fox_w_in[li], fox_b_f[li], fox_w_out[li])
        h = _mlp(h, norm_mlp[i].reshape(1, d), mlp_w1[i].astype(BF16), mlp_w2[i].astype(BF16),
                 gf, i == depth - 1, 1024, 512)
    return h.reshape(b, s, d)
```

```python
import functools
import math

import numpy as np
import jax
import jax.numpy as jnp
from jax import lax
from jax.experimental import pallas as pl
from jax.experimental.pallas import tpu as pltpu

F32 = jnp.float32
BF16 = jnp.bfloat16

D_MODEL = 1024
SEQ = 2048
N_HEADS = 16
HEAD_DIM = 64
N_GROUPS = 4
HPG = N_HEADS // N_GROUPS
CMP_BLOCK = 32
CMP_STRIDE = 16
N_CMP = (SEQ - CMP_BLOCK) // CMP_STRIDE + 1
N_CMP_PAD = 128
SEL_BLOCK = 64
N_SEL = SEQ // SEL_BLOCK
SEL_TOPK = 16
N_LOCAL_BLOCKS = 2
WINDOW = 512
REL_BUCKETS = 32
REL_MAX_DIST = 128
D_FF = 4 * D_MODEL
RMS_EPS = 1e-6
NEG = -1e30
FORCE_SCORE = 1e9

LANES = 128
TQ = 128
NQ = SEQ // TQ
WIN_TILES = WINDOW // TQ
PAIRS_PER_GROUP = HPG // 2
N_PAIRS = N_HEADS // 2
VMEM_LIMIT = 48 * 1024 * 1024

NSA_GROUP_COLS = 4 * LANES
NSA_KS_COL = 2 * LANES
NSA_KW_COL = 3 * LANES
NSA_KC_OFF = N_GROUPS * NSA_GROUP_COLS
NSA_VC_OFF = NSA_KC_OFF + N_GROUPS * HEAD_DIM
NSA_GATE_OFF = N_HEADS * HEAD_DIM + 6 * N_GROUPS * HEAD_DIM
C_PIECES = 3


def _nn(a, b):
    return jnp.dot(a, b, preferred_element_type=F32)


def _nt(a, b):
    return lax.dot_general(a, b, (((1,), (1,)), ((), ())), preferred_element_type=F32)


def _rms(x, g):
    ms = jnp.mean(x * x, axis=-1, keepdims=True)
    return x * lax.rsqrt(ms + RMS_EPS) * g


def _params(*sem):
    return pltpu.CompilerParams(dimension_semantics=sem, vmem_limit_bytes=VMEM_LIMIT)


def _norm_matmul_kernel(x_ref, g_ref, w_ref, o_ref, hn_ref, *, transposed):
    @pl.when(pl.program_id(1) == 0)
    def _():
        hn_ref[...] = _rms(x_ref[...], g_ref[...]).astype(BF16)

    if transposed:
        o_ref[...] = _nt(w_ref[...], hn_ref[...]).astype(o_ref.dtype)
    else:
        o_ref[...] = _nn(hn_ref[...], w_ref[...]).astype(o_ref.dtype)


def _norm_matmul(x, g, w, out_dtype, tm, tn, transposed=False):
    t, d = x.shape
    n = w.shape[0] if transposed else w.shape[1]
    if transposed:
        w_spec = pl.BlockSpec((tn, d), lambda i, j: (j, 0))
        o_spec = pl.BlockSpec((tn, tm), lambda i, j: (j, i))
        o_shape = jax.ShapeDtypeStruct((n, t), out_dtype)
    else:
        w_spec = pl.BlockSpec((d, tn), lambda i, j: (0, j))
        o_spec = pl.BlockSpec((tm, tn), lambda i, j: (i, j))
        o_shape = jax.ShapeDtypeStruct((t, n), out_dtype)
    return pl.pallas_call(
        functools.partial(_norm_matmul_kernel, transposed=transposed),
        grid=(t // tm, n // tn),
        in_specs=[
            pl.BlockSpec((tm, d), lambda i, j: (i, 0)),
            pl.BlockSpec((1, d), lambda i, j: (0, 0)),
            w_spec,
        ],
        out_specs=o_spec,
        out_shape=o_shape,
        scratch_shapes=[pltpu.VMEM((tm, d), BF16)],
        compiler_params=_params("parallel", "arbitrary"),
        name="norm_matmul_t" if transposed else "norm_matmul",
    )(x, g, w)


def _matmul_res_kernel(a_ref, w_ref, r_ref, o_ref):
    o_ref[...] = r_ref[...] + _nn(a_ref[...], w_ref[...])


def _matmul_res(a, w, res, tm, tn):
    t, k = a.shape
    n = w.shape[1]
    return pl.pallas_call(
        _matmul_res_kernel,
        grid=(t // tm, n // tn),
        in_specs=[
            pl.BlockSpec((tm, k), lambda i, j: (i, 0)),
            pl.BlockSpec((k, tn), lambda i, j: (0, j)),
            pl.BlockSpec((tm, tn), lambda i, j: (i, j)),
        ],
        out_specs=pl.BlockSpec((tm, tn), lambda i, j: (i, j)),
        out_shape=jax.ShapeDtypeStruct((t, n), F32),
        compiler_params=_params("parallel", "arbitrary"),
        name="matmul_res",
    )(a, w, res)


def _mlp_kernel(x_ref, g_ref, w1_ref, w2_ref, gf_ref, o_ref, hn_ref, acc_ref, *, final):
    j = pl.program_id(1)

    @pl.when(j == 0)
    def _():
        hn_ref[...] = _rms(x_ref[...], g_ref[...]).astype(BF16)
        acc_ref[...] = jnp.zeros_like(acc_ref)

    a = jnp.maximum(_nn(hn_ref[...], w1_ref[...]), 0.0)
    acc_ref[...] += _nn((a * a).astype(BF16), w2_ref[...])

    @pl.when(j == pl.num_programs(1) - 1)
    def _():
        y = x_ref[...] + acc_ref[...]
        if final:
            y = _rms(y, gf_ref[...])
        o_ref[...] = y


def _mlp(x, g, w1, w2, gf, final, tm, tf):
    t, d = x.shape
    f = w1.shape[1]
    return pl.pallas_call(
        functools.partial(_mlp_kernel, final=final),
        grid=(t // tm, f // tf),
        in_specs=[
            pl.BlockSpec((tm, d), lambda i, j: (i, 0)),
            pl.BlockSpec((1, d), lambda i, j: (0, 0)),
            pl.BlockSpec((d, tf), lambda i, j: (0, j)),
            pl.BlockSpec((tf, d), lambda i, j: (j, 0)),
            pl.BlockSpec((1, d), lambda i, j: (0, 0)),
        ],
        out_specs=pl.BlockSpec((tm, d), lambda i, j: (i, 0)),
        out_shape=jax.ShapeDtypeStruct((t, d), F32),
        scratch_shapes=[pltpu.VMEM((tm, d), BF16), pltpu.VMEM((tm, d), F32)],
        compiler_params=_params("parallel", "arbitrary"),
        name="mlp",
    )(x, g, w1, w2, gf)


def _gelu_tanh(x):
    return 0.5 * x * (1.0 + jnp.tanh(math.sqrt(2.0 / math.pi) * (x + 0.044715 * (x * x * x))))


def _compress_hidden(t_ref, pe_ref, w1_ref):
    t = t_ref[0, 0].astype(F32)
    xa = (t + pe_ref[0:1, :]).astype(BF16)
    xb = (t + pe_ref[1:2, :]).astype(BF16)
    top = _nn(xa, w1_ref[0])
    bot = _nn(xb, w1_ref[1])
    pre = top + pltpu.roll(bot, N_CMP_PAD - 1, axis=0)
    return _gelu_tanh(pre).astype(BF16)


def _compress_kernel(tk_ref, tv_ref, pek_ref, pev_ref, wk1_ref, wv1_ref, wk2_ref, wv2t_ref,
                     kc_ref, vct_ref):
    kc_ref[0, 0] = _nn(_compress_hidden(tk_ref, pek_ref, wk1_ref), wk2_ref[...]).astype(kc_ref.dtype)
    vct_ref[0, 0] = _nt(wv2t_ref[...], _compress_hidden(tv_ref, pev_ref, wv1_ref)).astype(vct_ref.dtype)


def _compress(tk, tv, pek, pev, wk1, wv1, wk2d, wv2t):
    b = tk.shape[0]
    chunk = CMP_STRIDE * HEAD_DIM
    t_spec = pl.BlockSpec((1, 1, N_CMP_PAD, chunk), lambda i, j: (i, j, 0, 0))
    pe_spec = pl.BlockSpec((2, chunk), lambda i, j: (0, 0))
    w1_spec = pl.BlockSpec((2, chunk, HEAD_DIM), lambda i, j: (0, 0, 0))
    return pl.pallas_call(
        _compress_kernel,
        grid=(b, N_GROUPS),
        in_specs=[t_spec, t_spec, pe_spec, pe_spec, w1_spec, w1_spec,
                  pl.BlockSpec((HEAD_DIM, LANES), lambda i, j: (0, 0)),
                  pl.BlockSpec((HEAD_DIM, HEAD_DIM), lambda i, j: (0, 0))],
        out_specs=[pl.BlockSpec((1, 1, N_CMP_PAD, LANES), lambda i, j: (i, j, 0, 0)),
                   pl.BlockSpec((1, 1, HEAD_DIM, N_CMP_PAD), lambda i, j: (i, j, 0, 0))],
        out_shape=[jax.ShapeDtypeStruct((b, N_GROUPS, N_CMP_PAD, LANES), BF16),
                   jax.ShapeDtypeStruct((b, N_GROUPS, HEAD_DIM, N_CMP_PAD), BF16)],
        compiler_params=_params("parallel", "arbitrary"),
        name="nsa_compress",
    )(tk, tv, pek, pev, wk1, wv1, wk2d, wv2t)


def _softmax_step(s, m, l):
    m_new = jnp.maximum(m, jnp.max(s, axis=0, keepdims=True))
    alpha = jnp.exp(m - m_new)
    p = jnp.exp(s - m_new)
    l_new = alpha * l + jnp.sum(p, axis=0, keepdims=True)
    return m_new, l_new, alpha, p.astype(BF16)


def _softmax_init(n):
    return jnp.full((1, n), -jnp.inf, F32), jnp.zeros((1, n), F32)


def _head_pair_rows(q, lo):
    return jnp.concatenate([jnp.where(lo, q, 0.0), jnp.where(lo, 0.0, q)], axis=0)


def _nsa_attn_kernel(pq_ref, vt_ref, kc_ref, vct_ref, bc_ref, bt_ref, gate_ref, epat_ref, ovt_ref,
                     o_ref):
    i = pl.program_id(2)
    r0 = pl.multiple_of(i * TQ, TQ)
    lo = lax.broadcasted_iota(jnp.int32, (TQ, LANES), 1) < HEAD_DIM
    n = 2 * TQ

    qp = [_head_pair_rows(pq_ref[pl.ds(r0, TQ), p * LANES:(p + 1) * LANES].astype(F32), lo).astype(BF16)
          for p in range(PAIRS_PER_GROUP)]

    o_cmp = []
    imp_t = jnp.zeros((N_SEL, TQ), F32)
    for p in range(PAIRS_PER_GROUP):
        s = _nt(kc_ref[0, 0], qp[p]) + bc_ref[0, 0, p]
        m = jnp.max(s, axis=0, keepdims=True)
        e = jnp.where(s > 0.5 * NEG, jnp.exp(s - m), 0.0)
        pr = e * (1.0 / jnp.maximum(jnp.sum(e, axis=0, keepdims=True), 1e-30))
        pb = pr.astype(BF16)
        o_cmp.append(_nn(vct_ref[0, 0], pb))
        ip = _nn(ovt_ref[...], pb)
        imp_t = imp_t + ip[:, 0:TQ] + ip[:, TQ:n]

    jidx = lax.broadcasted_iota(jnp.int32, (N_SEL, TQ), 0)
    tpos = i * TQ + lax.broadcasted_iota(jnp.int32, (N_SEL, TQ), 1)
    rel = (tpos >> int(math.log2(SEL_BLOCK))) - jidx
    forced = (jidx == 0) | ((rel >= 0) & (rel < N_LOCAL_BLOCKS))
    score = jnp.where(rel >= 0, jnp.where(forced, FORCE_SCORE, imp_t), NEG)
    rank = jnp.zeros((N_SEL, TQ), F32)
    for jp in range(N_SEL):
        sj = score[jp:jp + 1, :]
        before = (sj > score) | ((sj == score) & (jidx > jp))
        rank = rank + jnp.where(before, 1.0, 0.0)
    drop_t = jnp.where(rank < SEL_TOPK, 0.0, -1.0)
    drop_t = jnp.concatenate([drop_t, jnp.zeros((TQ - N_SEL, TQ), F32)], axis=0).astype(BF16)
    eye = (lax.broadcasted_iota(jnp.int32, (TQ, TQ), 0)
           == lax.broadcasted_iota(jnp.int32, (TQ, TQ), 1)).astype(F32).astype(BF16)
    drop = _nt(eye, drop_t).astype(BF16)
    dropx = jnp.concatenate([drop, drop], axis=0)

    o_sel, o_win = [], []
    for p in range(PAIRS_PER_GROUP):
        qx = jnp.concatenate([qp[p], dropx], axis=1)

        def sel_body(kt, carry, p=p, qx=qx):
            m, l, acc = carry
            k0 = pl.multiple_of(kt * TQ, TQ)
            kx = jnp.concatenate([pq_ref[pl.ds(k0, TQ), NSA_KS_COL:NSA_KS_COL + LANES],
                                  epat_ref[pl.ds(k0, TQ), :]], axis=1)
            s = _nt(kx, qx) + bt_ref[0, jnp.minimum(i - kt, 2), p]
            m, l, alpha, pb = _softmax_step(s, m, l)
            acc = alpha * acc + _nn(vt_ref[0:HEAD_DIM, pl.ds(k0, TQ)], pb)
            return m, l, acc

        m, l, acc = lax.fori_loop(0, i + 1, sel_body,
                                  _softmax_init(n) + (jnp.zeros((HEAD_DIM, n), F32),))
        o_sel.append(acc * (1.0 / l))

        def win_body(kt, carry, p=p):
            m, l, acc = carry
            k0 = pl.multiple_of(kt * TQ, TQ)
            d = i - kt
            s = (_nt(pq_ref[pl.ds(k0, TQ), NSA_KW_COL:NSA_KW_COL + LANES], qp[p])
                 + bt_ref[0, jnp.where(d == WIN_TILES, 3, jnp.minimum(d, 2)), p])
            m, l, alpha, pb = _softmax_step(s, m, l)
            acc = alpha * acc + _nn(vt_ref[HEAD_DIM:2 * HEAD_DIM, pl.ds(k0, TQ)], pb)
            return m, l, acc

        m, l, acc = lax.fori_loop(jnp.maximum(i - WIN_TILES, 0), i + 1, win_body,
                                  _softmax_init(n) + (jnp.zeros((HEAD_DIM, n), F32),))
        o_win.append(acc * (1.0 / l))

    gs = 1.0 / (1.0 + jnp.exp(-gate_ref[0, 0]))
    for p in range(PAIRS_PER_GROUP):
        outs = []
        for hh in range(2):
            h = 2 * p + hh
            cols = slice(hh * TQ, (hh + 1) * TQ)
            outs.append(gs[h:h + 1] * o_cmp[p][:, cols] + gs[HPG + h:HPG + h + 1] * o_sel[p][:, cols]
                        + gs[2 * HPG + h:2 * HPG + h + 1] * o_win[p][:, cols])
        ot = jnp.concatenate(outs, axis=0)
        o_ref[:, p * LANES:(p + 1) * LANES] = ot.T.astype(o_ref.dtype)


def _nsa_attn(proj, vt, kcd, vct, bias_c, bias_t, gates, epat, ovt):
    b = kcd.shape[0]
    n = 2 * TQ
    return pl.pallas_call(
        _nsa_attn_kernel,
        grid=(N_GROUPS, b, NQ),
        in_specs=[
            pl.BlockSpec((SEQ, NSA_GROUP_COLS), lambda g, bi, i: (bi, g)),
            pl.BlockSpec((2 * HEAD_DIM, SEQ), lambda g, bi, i: (g, bi)),
            pl.BlockSpec((1, 1, N_CMP_PAD, LANES), lambda g, bi, i: (bi, g, 0, 0)),
            pl.BlockSpec((1, 1, HEAD_DIM, N_CMP_PAD), lambda g, bi, i: (bi, g, 0, 0)),
            pl.BlockSpec((1, 1, PAIRS_PER_GROUP, N_CMP_PAD, n), lambda g, bi, i: (g, i, 0, 0, 0)),
            pl.BlockSpec((1, 4, PAIRS_PER_GROUP, TQ, n), lambda g, bi, i: (g, 0, 0, 0, 0)),
            pl.BlockSpec((1, 1, 3 * HPG, TQ), lambda g, bi, i: (bi, g, 0, i)),
            pl.BlockSpec((SEQ, LANES), lambda g, bi, i: (0, 0)),
            pl.BlockSpec((N_SEL, N_CMP_PAD), lambda g, bi, i: (0, 0)),
        ],
        out_specs=pl.BlockSpec((TQ, HPG * HEAD_DIM), lambda g, bi, i: (bi * NQ + i, g)),
        out_shape=jax.ShapeDtypeStruct((b * SEQ, N_HEADS * HEAD_DIM), BF16),
        compiler_params=_params("parallel", "parallel", "arbitrary"),
        name="nsa_attn",
    )(proj, vt, kcd, vct, bias_c, bias_t, gates, epat, ovt)


def _split3(x):
    hi = x.astype(BF16)
    r1 = x - hi.astype(F32)
    mid = r1.astype(BF16)
    low = (r1 - mid.astype(F32)).astype(BF16)
    return hi, mid, low


def _fox_cumsum_kernel(f_ref, b_ref, place_ref, c_ref):
    x = f_ref[...] + b_ref[...]
    ls = -(jnp.maximum(-x, 0.0) + jnp.log1p(jnp.exp(-jnp.abs(x))))
    tri = (lax.broadcasted_iota(jnp.int32, (LANES, LANES), 0)
           >= lax.broadcasted_iota(jnp.int32, (LANES, LANES), 1)).astype(F32).astype(BF16)
    carry = jnp.zeros((1, LANES), F32)
    for blk in range(SEQ // LANES):
        cs = carry + sum(_nn(tri, piece) for piece in _split3(ls[blk * LANES:(blk + 1) * LANES]))
        carry = cs[LANES - 1:LANES, :]
        out = sum(_nn(piece, place_ref[k]) for k, piece in enumerate(_split3(cs)))
        c_ref[blk * LANES:(blk + 1) * LANES, :] = out.astype(c_ref.dtype)


def _fox_cumsum(fl, bf, place, b):
    return pl.pallas_call(
        _fox_cumsum_kernel,
        grid=(b,),
        in_specs=[pl.BlockSpec((SEQ, LANES), lambda i: (i, 0)),
                  pl.BlockSpec((1, LANES), lambda i: (0, 0)),
                  pl.BlockSpec((C_PIECES, LANES, N_PAIRS * LANES), lambda i: (0, 0, 0))],
        out_specs=pl.BlockSpec((SEQ, N_PAIRS * LANES), lambda i: (i, 0)),
        out_shape=jax.ShapeDtypeStruct((b * SEQ, N_PAIRS * LANES), BF16),
        compiler_params=_params("parallel"),
        name="fox_cumsum",
    )(fl, bf, place)


def _fox_place_table():
    place = np.zeros((C_PIECES, LANES, N_PAIRS * LANES), np.float32)
    for hd in range(N_HEADS):
        for k in range(C_PIECES):
            lane = (hd // 2) * LANES + C_PIECES * (hd % 2) + k
            place[k, hd, lane] = 1.0
            place[k, hd, lane + 2 * C_PIECES] = 1.0
    return jnp.asarray(place, BF16)


def _fox_attn_kernel(q_ref, k_ref, vt_ref, ca_ref, o_ref, kx_ref):
    i = pl.program_id(2)
    r0 = pl.multiple_of(i * TQ, TQ)
    np_ = C_PIECES

    @pl.when(i == 0)
    def _():
        lane = lax.broadcasted_iota(jnp.int32, (SEQ, LANES), 1)
        a = ca_ref[...].astype(F32)
        kx_ref[:, 0:LANES] = k_ref[...]
        kx_ref[:, LANES:2 * LANES] = jnp.where(
            lane < 2 * np_, -a, jnp.where(lane < 4 * np_, 1.0, 0.0)).astype(BF16)

    lane = lax.broadcasted_iota(jnp.int32, (TQ, LANES), 1)
    lo = lane < HEAD_DIM
    a = ca_ref[pl.ds(r0, TQ), :].astype(F32)
    qa0 = jnp.where(lane < np_, 1.0, jnp.where((lane >= 2 * np_) & (lane < 3 * np_), a, 0.0))
    qa1 = jnp.where((lane >= np_) & (lane < 2 * np_), 1.0,
                    jnp.where((lane >= 3 * np_) & (lane < 4 * np_), a, 0.0))
    qx = jnp.concatenate([_head_pair_rows(q_ref[...].astype(F32), lo),
                          jnp.concatenate([qa0, qa1], axis=0)], axis=1).astype(BF16)
    n = 2 * TQ

    def step(kt, carry, masked):
        m, l, acc0, acc1 = carry
        k0 = pl.multiple_of(kt * TQ, TQ)
        s = _nt(kx_ref[pl.ds(k0, TQ), :], qx)
        if masked:
            key = lax.broadcasted_iota(jnp.int32, (TQ, n), 0)
            qry = lax.broadcasted_iota(jnp.int32, (TQ, n), 1) & (TQ - 1)
            s = jnp.where(key <= qry, s, NEG)
        m, l, alpha, pb = _softmax_step(s, m, l)
        acc0 = alpha[:, 0:TQ] * acc0 + _nn(vt_ref[0:HEAD_DIM, pl.ds(k0, TQ)], pb[:, 0:TQ])
        acc1 = alpha[:, TQ:n] * acc1 + _nn(vt_ref[HEAD_DIM:2 * HEAD_DIM, pl.ds(k0, TQ)], pb[:, TQ:n])
        return m, l, acc0, acc1

    zero = jnp.zeros((HEAD_DIM, TQ), F32)
    carry = lax.fori_loop(0, i, functools.partial(step, masked=False), _softmax_init(n) + (zero, zero))
    m, l, acc0, acc1 = step(i, carry, True)
    rl = 1.0 / l
    ot = jnp.concatenate([acc0 * rl[:, 0:TQ], acc1 * rl[:, TQ:n]], axis=0)
    o_ref[...] = ot.T.astype(o_ref.dtype)


def _fox_attn(qk, vt, caug, b):
    return pl.pallas_call(
        _fox_attn_kernel,
        grid=(b, N_PAIRS, NQ),
        in_specs=[
            pl.BlockSpec((TQ, LANES), lambda bi, hp, i: (bi * NQ + i, hp)),
            pl.BlockSpec((SEQ, LANES), lambda bi, hp, i: (bi, N_PAIRS + hp)),
            pl.BlockSpec((2 * HEAD_DIM, SEQ), lambda bi, hp, i: (hp, bi)),
            pl.BlockSpec((SEQ, LANES), lambda bi, hp, i: (bi, hp)),
        ],
        out_specs=pl.BlockSpec((TQ, LANES), lambda bi, hp, i: (bi * NQ + i, hp)),
        out_shape=jax.ShapeDtypeStruct((b * SEQ, N_HEADS * HEAD_DIM), BF16),
        scratch_shapes=[pltpu.VMEM((SEQ, 2 * LANES), BF16)],
        compiler_params=_params("parallel", "parallel", "arbitrary"),
        name="fox_attn",
    )(qk, qk, vt, caug)


def _rel_bucket(dist):
    n = jnp.maximum(dist, 0)
    max_exact = REL_BUCKETS // 2
    nf = jnp.maximum(n, 1).astype(F32)
    large = max_exact + (jnp.log(nf / max_exact) / math.log(REL_MAX_DIST / max_exact)
                         * (REL_BUCKETS - max_exact)).astype(jnp.int32)
    large = jnp.minimum(large, REL_BUCKETS - 1)
    return jnp.where(n < max_exact, n, large)


def _skew(w, rows, step, cols):
    length = w.shape[1]
    flat = jnp.tile(w, (1, rows))[:, :rows * (length - step)]
    return flat.reshape(w.shape[0], rows, length - step)[:, :, :cols]


def _bias_tables(rel_bias):
    fb = rel_bias[_rel_bucket(jnp.arange(SEQ))].astype(F32).T
    neg = lambda k: jnp.full((N_HEADS, k), NEG, F32)
    end = CMP_BLOCK - 1
    w = jnp.concatenate([neg(end), fb[:, :SEQ - end], neg(SEQ)], axis=1)
    bc = _skew(w, N_CMP_PAD, CMP_STRIDE, SEQ)
    bc = bc.reshape(N_GROUPS, PAIRS_PER_GROUP, 2, N_CMP_PAD, NQ, TQ)
    bias_c = bc.transpose(0, 4, 1, 3, 2, 5).reshape(N_GROUPS, NQ, PAIRS_PER_GROUP, N_CMP_PAD, 2 * TQ)
    t0 = _skew(jnp.concatenate([fb[:, :TQ], neg(TQ)], axis=1), TQ, 1, TQ)
    t1 = _skew(jnp.concatenate([fb[:, TQ:2 * TQ], fb[:, :TQ]], axis=1), TQ, 1, TQ)
    far = jnp.broadcast_to(fb[:, SEQ - 1][:, None, None], (N_HEADS, TQ, TQ))
    key = jnp.arange(TQ)[:, None]
    qry = jnp.arange(TQ)[None, :]
    t3 = jnp.where(qry < key, far, NEG)
    bt = jnp.stack([t0, t1, far, t3], axis=0).reshape(4, N_GROUPS, PAIRS_PER_GROUP, 2, TQ, TQ)
    bias_t = bt.transpose(1, 0, 2, 4, 3, 5).reshape(N_GROUPS, 4, PAIRS_PER_GROUP, TQ, 2 * TQ)
    return bias_c, bias_t


def _nsa_const_tables():
    k = np.arange(SEQ)[:, None]
    j = np.arange(LANES)[None, :]
    epat = np.where(k // SEL_BLOCK == j, -NEG, 0.0).astype(np.float32)
    c = np.arange(N_CMP_PAD)[None, :]
    jj = np.arange(N_SEL)[:, None]
    per = SEL_BLOCK // CMP_STRIDE
    ov = ((c // per == jj).astype(np.float32) + ((c + 1) // per == jj).astype(np.float32))
    ov = np.where(c < N_CMP, ov, 0.0)
    return jnp.asarray(epat, BF16), jnp.asarray(ov, BF16)


def _nsa_w_in_layout(w_in):
    scale = HEAD_DIM ** -0.5
    hd = HEAD_DIM
    kv_off = N_HEADS * hd

    def kv(idx, g):
        o = kv_off + idx * N_GROUPS * hd + g * hd
        return w_in[:, o:o + hd]

    parts = []
    for g in range(N_GROUPS):
        parts.append(w_in[:, g * HPG * hd:(g + 1) * HPG * hd] * scale)
        parts += [kv(2, g), kv(2, g), kv(4, g), kv(4, g)]
    parts += [kv(0, g) for g in range(N_GROUPS)] + [kv(1, g) for g in range(N_GROUPS)]
    w_main = jnp.concatenate(parts, axis=1).astype(BF16)
    w_vt = jnp.concatenate([kv(idx, g) for g in range(N_GROUPS) for idx in (3, 5)],
                           axis=1).T.astype(BF16)
    w_gate = jnp.pad(w_in[:, NSA_GATE_OFF:], ((0, 0), (0, LANES - 3 * N_HEADS))).astype(BF16)
    return w_main, w_vt, w_gate


def _nsa_layer(x, b, g_mix, rel_bias, w_in, pe_k, wk1, wk2, pe_v, wv1, wv2, w_out):
    w_main, w_vt, w_gate = _nsa_w_in_layout(w_in)
    proj = _norm_matmul(x, g_mix, w_main, BF16, 1024, 512)
    vt = _norm_matmul(x, g_mix, w_vt, BF16, 1024, 512, transposed=True)
    gl = _norm_matmul(x, g_mix, w_gate, F32, 1024, LANES)

    chunk = CMP_STRIDE * HEAD_DIM

    def chunks(off):
        t = proj[:, off:off + N_GROUPS * HEAD_DIM].reshape(b, SEQ, N_GROUPS, HEAD_DIM)
        return t.transpose(0, 2, 1, 3).reshape(b, N_GROUPS, N_CMP_PAD, chunk)

    kcd, vct = _compress(
        chunks(NSA_KC_OFF), chunks(NSA_VC_OFF),
        pe_k.reshape(2, chunk), pe_v.reshape(2, chunk),
        wk1.reshape(2, chunk, HEAD_DIM).astype(BF16), wv1.reshape(2, chunk, HEAD_DIM).astype(BF16),
        jnp.concatenate([wk2, wk2], axis=1).astype(BF16), wv2.T.astype(BF16))

    gates = gl[:, :3 * N_HEADS].reshape(b, SEQ, 3, N_GROUPS, HPG)
    gates = gates.transpose(0, 3, 2, 4, 1).reshape(b, N_GROUPS, 3 * HPG, SEQ)
    bias_c, bias_t = _bias_tables(rel_bias)
    epat, ovt = _nsa_const_tables()
    o = _nsa_attn(proj, vt, kcd, vct, bias_c, bias_t, gates, epat, ovt)
    return _matmul_res(o, w_out.astype(BF16), x, 1024, 512)


def _fox_layer(x, b, g_mix, w_in, b_f, w_out):
    hd = N_HEADS * HEAD_DIM
    w_qk = jnp.concatenate([w_in[:, :hd] * HEAD_DIM ** -0.5, w_in[:, hd:2 * hd]], axis=1).astype(BF16)
    w_vt = w_in[:, 2 * hd:3 * hd].T.astype(BF16)
    w_f = jnp.pad(w_in[:, 3 * hd:], ((0, 0), (0, LANES - N_HEADS))).astype(BF16)
    qk = _norm_matmul(x, g_mix, w_qk, BF16, 1024, 512)
    vt = _norm_matmul(x, g_mix, w_vt, BF16, 1024, 512, transposed=True)
    fl = _norm_matmul(x, g_mix, w_f, F32, 1024, LANES)
    bf = jnp.pad(b_f.astype(F32), (0, LANES - N_HEADS)).reshape(1, LANES)
    caug = _fox_cumsum(fl, bf, _fox_place_table(), b)
    o = _fox_attn(qk, vt, caug, b)
    return _matmul_res(o, w_out.astype(BF16), x, 1024, 512)


def kernel(x, rel_bias, norm_mix, norm_mlp, nsa_w_in, nsa_pe_k, nsa_wk1, nsa_wk2, nsa_pe_v,
           nsa_wv1, nsa_wv2, nsa_w_out, fox_w_in, fox_b_f, fox_w_out, mlp_w1, mlp_w2, final_norm):
    b, s, d = x.shape
    assert s == SEQ and d == D_MODEL
    depth = norm_mix.shape[0]
    h = x.reshape(b * s, d)
    gf = final_norm.reshape(1, d)
    for i in range(depth):
        li = i // 2
        g_mix = norm_mix[i].reshape(1, d)
        if i % 2 == 0:
            h = _nsa_layer(h, b, g_mix, rel_bias, nsa_w_in[li], nsa_pe_k[li], nsa_wk1[li],
                           nsa_wk2[li], nsa_pe_v[li], nsa_wv1[li], nsa_wv2[li], nsa_w_out[li])
        else:
            h = _fox_layer(h, b, g_mix, fox_w_in[li], fox_b_f[li], fox_w_out[li])
        h = _mlp(h, norm_mlp[i].reshape(1, d), mlp_w1[i].astype(BF16), mlp_w2[i].astype(BF16),
                 gf, i == depth - 1, 1024, 512)
    return h.reshape(b, s, d)
```
